```python
import jax, jax.numpy as jnp
from jax import lax
import numpy as np

D_MODEL = 1024
BATCH = 32
SEQ = 2048
DEPTH = 2

N_BRANCH = 4
BRANCH_W = D_MODEL // 2
N_GROUPS = 4
GROUP_W = BRANCH_W // N_GROUPS
POOL_WINDOWS = (2, 4, 8, 16)
CONV_K = 31
SHORT_K = 3
CHUNK = 128
N_PIECES = 12
N_BRANCH_COLS = N_PIECES * BRANCH_W
IN_COLS = N_BRANCH_COLS + N_BRANCH * D_MODEL
RMS_EPS = 1e-6
LN_EPS = 1e-5

kernel_name = "hybrid_gated_parallel_mixers"


def rms_norm(x, g):
    xf = x.astype(jnp.float32)
    y = xf * lax.rsqrt(jnp.mean(xf * xf, axis=-1, keepdims=True) + RMS_EPS)
    return (y * g.astype(jnp.float32)).astype(x.dtype)


def layer_norm(x, g, b):
    xf = x.astype(jnp.float32)
    mu = jnp.mean(xf, axis=-1, keepdims=True)
    var = jnp.mean(jnp.square(xf - mu), axis=-1, keepdims=True)
    y = (xf - mu) * lax.rsqrt(var + LN_EPS)
    return (y * g.astype(jnp.float32) + b.astype(jnp.float32)).astype(x.dtype)


def causal_dwconv(x, w):
    k, c = w.shape
    return lax.conv_general_dilated(
        x, w[:, None, :].astype(x.dtype), window_strides=(1,), padding=[(k - 1, 0)],
        dimension_numbers=("NWC", "WIO", "NWC"), feature_group_count=c)


def pool_mixer(xa, pool_w, pool_scale):
    b, s, _ = xa.shape
    xf = xa.astype(jnp.float32)
    csum = jnp.cumsum(xf, axis=1)
    t = jnp.arange(1, s + 1, dtype=jnp.float32)
    groups = []
    for j, win in enumerate(POOL_WINDOWS):
        cj = csum[..., j * GROUP_W:(j + 1) * GROUP_W]
        prev = jnp.pad(cj, ((0, 0), (win, 0), (0, 0)))[:, :s]
        mean = (cj - prev) / jnp.minimum(t, float(win))[None, :, None]
        groups.append(mean - xf[..., j * GROUP_W:(j + 1) * GROUP_W])
    pooled = jnp.stack(groups, axis=2)
    mixed = jnp.einsum("bsgc,gcd->bsgd", pooled, pool_w.astype(jnp.float32))
    return (mixed.reshape(b, s, BRANCH_W) * pool_scale.astype(jnp.float32)).astype(xa.dtype)


def conformer_conv(a, gb, conv_w, conv_b, ln_g, ln_b):
    y = a * jax.nn.sigmoid(gb)
    y = causal_dwconv(y, conv_w) + conv_b.astype(y.dtype)
    return jax.nn.silu(layer_norm(y, ln_g, ln_b))


def spatial_gating(u, v, ln_g, ln_b, sgu_w, sgu_b):
    b, s, _ = u.shape
    v = layer_norm(v, ln_g, ln_b).reshape(b, s // CHUNK, CHUNK, N_GROUPS, GROUP_W)
    mask = jnp.tril(jnp.ones((CHUNK, CHUNK), dtype=v.dtype))
    ws = sgu_w.astype(v.dtype) * mask[None]
    sp = jnp.einsum("gts,bnsgc->bntgc", ws, v) + sgu_b.T.astype(v.dtype)[None, None, :, :, None]
    return u * sp.reshape(b, s, BRANCH_W)


def short_gated_conv(bg, cg, xs, sc_w):
    return bg * causal_dwconv(cg * xs, sc_w)


def setup_inputs(seed: int = 0) -> dict:
    key = jax.random.key(seed)
    ks = jax.random.split(key, 20)
    f32 = jnp.float32
    nrm = lambda k, shape, scale: jax.random.normal(k, shape, f32) * scale
    return {
        "x": jax.random.normal(ks[0], (BATCH, SEQ, D_MODEL), f32),
        "norm_g": 1.0 + nrm(ks[1], (DEPTH, D_MODEL), 0.02),
        "w_in": nrm(ks[2], (DEPTH, D_MODEL, IN_COLS), D_MODEL ** -0.5),
        "pool_w": nrm(ks[3], (DEPTH, N_GROUPS, GROUP_W, GROUP_W), GROUP_W ** -0.5),
        "pool_scale": 1.0 + nrm(ks[4], (DEPTH, BRANCH_W), 0.02),
        "conv_w": nrm(ks[5], (DEPTH, CONV_K, BRANCH_W), CONV_K ** -0.5),
        "conv_b": nrm(ks[6], (DEPTH, BRANCH_W), 0.01),
        "conv_ln_g": 1.0 + nrm(ks[7], (DEPTH, BRANCH_W), 0.02),
        "conv_ln_b": nrm(ks[8], (DEPTH, BRANCH_W), 0.01),
        "sgu_ln_g": 1.0 + nrm(ks[9], (DEPTH, BRANCH_W), 0.02),
        "sgu_ln_b": nrm(ks[10], (DEPTH, BRANCH_W), 0.01),
        "sgu_w": nrm(ks[11], (DEPTH, N_GROUPS, CHUNK, CHUNK), CHUNK ** -0.5),
        "sgu_b": 1.0 + nrm(ks[12], (DEPTH, N_GROUPS, CHUNK), 0.01),
        "sc_w": nrm(ks[13], (DEPTH, SHORT_K, BRANCH_W), SHORT_K ** -0.5),
        "w_branch": nrm(ks[14], (DEPTH, N_BRANCH, BRANCH_W, D_MODEL), BRANCH_W ** -0.5),
        "w_o": nrm(ks[15], (DEPTH, D_MODEL, D_MODEL), D_MODEL ** -0.5),
        "final_g": 1.0 + nrm(ks[16], (D_MODEL,), 0.02),
    }


def reference(x, norm_g, w_in, pool_w, pool_scale, conv_w, conv_b, conv_ln_g, conv_ln_b,
              sgu_ln_g, sgu_ln_b, sgu_w, sgu_b, sc_w, w_branch, w_o, final_g):
    b, s, d = x.shape
    for l in range(DEPTH):
        h = rms_norm(x, norm_g[l])
        proj = jnp.einsum("bsd,dk->bsk", h, w_in[l].astype(h.dtype))
        (p_x, p_gate, c_a, c_b, c_gate, g_u, g_v, g_gate,
         s_b, s_c, s_x, s_gate) = jnp.split(proj[..., :N_BRANCH_COLS], N_PIECES, axis=-1)
        merge_gates = jax.nn.sigmoid(proj[..., N_BRANCH_COLS:].reshape(b, s, N_BRANCH, d))

        z_pool = pool_mixer(p_x, pool_w[l], pool_scale[l]) * jax.nn.silu(p_gate)
        z_conv = conformer_conv(c_a, c_b, conv_w[l], conv_b[l], conv_ln_g[l], conv_ln_b[l]) * jax.nn.silu(c_gate)
        z_sgu = spatial_gating(g_u, g_v, sgu_ln_g[l], sgu_ln_b[l], sgu_w[l], sgu_b[l]) * jax.nn.silu(g_gate)
        z_sc = short_gated_conv(s_b, s_c, s_x, sc_w[l]) * jax.nn.silu(s_gate)

        z = jnp.stack([z_pool, z_conv, z_sgu, z_sc], axis=2)
        branch_out = jnp.einsum("bsnc,ncd->bsnd", z, w_branch[l].astype(z.dtype))
        merged = jnp.sum(merge_gates * branch_out, axis=2)
        x = x + jnp.einsum("bsd,de->bse", merged, w_o[l].astype(merged.dtype))
    return rms_norm(x, final_g)
```

```python
import functools

import jax
import jax.numpy as jnp
from jax import lax
from jax.experimental import pallas as pl
from jax.experimental.pallas import tpu as pltpu

D_MODEL = 1024
BRANCH_W = 512
N_BRANCH = 4
N_GROUPS = 4
GROUP_W = 128
POOL_WINDOWS = (2, 4, 8, 16)
CONV_K = 31
SHORT_K = 3
CHUNK = 128
N_BRANCH_COLS = 12 * BRANCH_W
IN_COLS = N_BRANCH_COLS + N_BRANCH * D_MODEL
RMS_EPS = 1e-6
LN_EPS = 1e-5

POOL_COL = 0 * BRANCH_W
CONV_COL = 2 * BRANCH_W
SGU_COL = 5 * BRANCH_W
SC_COL = 8 * BRANCH_W

TILE = 256
ROWS = 32
CONV_HALO = 32
POOL_HALO = 16
SC_HALO = 8
VMEM_LIMIT_BYTES = 56 * 1024 * 1024

F32 = jnp.float32
BF16 = jnp.bfloat16


def _silu(v):
    return v * jax.nn.sigmoid(v)


def _layer_norm_rows(v, g, b):
    mu = jnp.mean(v, axis=-1, keepdims=True)
    d = v - mu
    var = jnp.mean(d * d, axis=-1, keepdims=True)
    return d * lax.rsqrt(var + LN_EPS) * g + b


def _rms_norm_rows(v, g):
    ms = jnp.mean(v * v, axis=-1, keepdims=True)
    return v * lax.rsqrt(ms + RMS_EPS) * g


def _dot(a, b):
    return jnp.dot(a, b, preferred_element_type=F32)


def _layer_kernel(x_ref, ng_ref, win_ref, poolw_ref, pscale_ref, convw_ref, convb_ref,
                  clng_ref, clnb_ref, slng_ref, slnb_ref, sguw_ref, sgub_ref, scw_ref,
                  wbr_ref, wo_ref, fg_ref, o_ref,
                  hb_ref, proj_ref, z_ref, mg_ref, ybuf, pbuf, l1buf, l2buf, l3buf,
                  mbuf, vb_ref, poolb_ref, *, final_norm):
    j = pl.program_id(1)
    chunks = range(0, TILE, ROWS)

    @pl.when(j == 0)
    def _zero_history():
        ybuf[0:CONV_HALO, :] = jnp.zeros((CONV_HALO, BRANCH_W), F32)
        pbuf[0:POOL_HALO, :] = jnp.zeros((POOL_HALO, BRANCH_W), F32)
        l1buf[0:POOL_HALO, :] = jnp.zeros((POOL_HALO, 4 * GROUP_W), F32)
        l2buf[0:POOL_HALO, :] = jnp.zeros((POOL_HALO, 3 * GROUP_W), F32)
        l3buf[0:POOL_HALO, :] = jnp.zeros((POOL_HALO, 2 * GROUP_W), F32)
        mbuf[0:SC_HALO, :] = jnp.zeros((SC_HALO, BRANCH_W), F32)

    for r0 in chunks:
        xr = x_ref[r0:r0 + ROWS, :]
        hb_ref[r0:r0 + ROWS, :] = _rms_norm_rows(xr, ng_ref[...]).astype(BF16)

    pbuf[POOL_HALO:POOL_HALO + TILE, :] = _dot(
        hb_ref[...], win_ref[:, POOL_COL:POOL_COL + BRANCH_W])
    proj_ref[:, 0:BRANCH_W] = _dot(
        hb_ref[...], win_ref[:, POOL_COL + BRANCH_W:POOL_COL + 2 * BRANCH_W])
    for r0 in chunks:
        r = POOL_HALO + r0
        l1buf[r:r + ROWS, :] = pbuf[r:r + ROWS, :] + pbuf[r - 1:r - 1 + ROWS, :]
    for r0 in chunks:
        r = POOL_HALO + r0
        l2buf[r:r + ROWS, :] = (l1buf[r:r + ROWS, GROUP_W:4 * GROUP_W]
                                + l1buf[r - 2:r - 2 + ROWS, GROUP_W:4 * GROUP_W])
    for r0 in chunks:
        r = POOL_HALO + r0
        l3buf[r:r + ROWS, :] = (l2buf[r:r + ROWS, GROUP_W:3 * GROUP_W]
                                + l2buf[r - 4:r - 4 + ROWS, GROUP_W:3 * GROUP_W])
    for r0 in chunks:
        r = POOL_HALO + r0
        pos = (j * TILE + r0 + 1
               + lax.broadcasted_iota(jnp.int32, (ROWS, GROUP_W), 0)).astype(F32)
        sums = (
            l1buf[r:r + ROWS, 0:GROUP_W],
            l2buf[r:r + ROWS, 0:GROUP_W],
            l3buf[r:r + ROWS, 0:GROUP_W],
            l3buf[r:r + ROWS, GROUP_W:2 * GROUP_W] + l3buf[r - 8:r - 8 + ROWS, GROUP_W:2 * GROUP_W],
        )
        for g, win in enumerate(POOL_WINDOWS):
            cols = slice(g * GROUP_W, (g + 1) * GROUP_W)
            mean = sums[g] / jnp.minimum(pos, float(win))
            poolb_ref[r0:r0 + ROWS, cols] = (mean - pbuf[r:r + ROWS, cols]).astype(BF16)
    for g in range(N_GROUPS):
        cols = slice(g * GROUP_W, (g + 1) * GROUP_W)
        proj_ref[:, BRANCH_W + g * GROUP_W:BRANCH_W + (g + 1) * GROUP_W] = _dot(
            poolb_ref[:, cols], poolw_ref[g])
    for r0 in chunks:
        rows = slice(r0, r0 + ROWS)
        mixed = proj_ref[rows, BRANCH_W:2 * BRANCH_W]
        gate = proj_ref[rows, 0:BRANCH_W]
        z_ref[0, rows, :] = (mixed * pscale_ref[...] * _silu(gate)).astype(BF16)

    proj_ref[:, 0:3 * BRANCH_W] = _dot(
        hb_ref[...], win_ref[:, CONV_COL:CONV_COL + 3 * BRANCH_W])
    for r0 in chunks:
        rows = slice(r0, r0 + ROWS)
        a = proj_ref[rows, 0:BRANCH_W]
        b = proj_ref[rows, BRANCH_W:2 * BRANCH_W]
        ybuf[CONV_HALO + r0:CONV_HALO + r0 + ROWS, :] = a * jax.nn.sigmoid(b)
    for r0 in chunks:
        rows = slice(r0, r0 + ROWS)
        base = CONV_HALO + r0 - (CONV_K - 1)
        acc = convw_ref[0:1, :] * ybuf[base:base + ROWS, :]
        for k in range(1, CONV_K):
            acc = acc + convw_ref[k:k + 1, :] * ybuf[base + k:base + k + ROWS, :]
        acc = acc + convb_ref[...]
        y = _silu(_layer_norm_rows(acc, clng_ref[...], clnb_ref[...]))
        gate = proj_ref[rows, 2 * BRANCH_W:3 * BRANCH_W]
        z_ref[1, rows, :] = (y * _silu(gate)).astype(BF16)

    proj_ref[:, 0:3 * BRANCH_W] = _dot(
        hb_ref[...], win_ref[:, SGU_COL:SGU_COL + 3 * BRANCH_W])
    for r0 in chunks:
        rows = slice(r0, r0 + ROWS)
        v = proj_ref[rows, BRANCH_W:2 * BRANCH_W]
        vb_ref[rows, :] = _layer_norm_rows(v, slng_ref[...], slnb_ref[...]).astype(BF16)
    tri = (lax.broadcasted_iota(jnp.int32, (CHUNK, CHUNK), 0)
           >= lax.broadcasted_iota(jnp.int32, (CHUNK, CHUNK), 1))
    for g in range(N_GROUPS):
        cols = slice(g * GROUP_W, (g + 1) * GROUP_W)
        wmask = jnp.where(tri, sguw_ref[g], 0.0).astype(BF16)
        for n in range(TILE // CHUNK):
            rows = slice(n * CHUNK, (n + 1) * CHUNK)
            sp = _dot(wmask, vb_ref[rows, cols]) + sgub_ref[:, cols]
            u = proj_ref[rows, cols]
            gate = proj_ref[rows, 2 * BRANCH_W + g * GROUP_W:2 * BRANCH_W + (g + 1) * GROUP_W]
            z_ref[2, rows, cols] = (u * sp * _silu(gate)).astype(BF16)

    proj_ref[:, 0:4 * BRANCH_W] = _dot(
        hb_ref[...], win_ref[:, SC_COL:SC_COL + 4 * BRANCH_W])
    for r0 in chunks:
        rows = slice(r0, r0 + ROWS)
        mbuf[SC_HALO + r0:SC_HALO + r0 + ROWS, :] = (
            proj_ref[rows, BRANCH_W:2 * BRANCH_W] * proj_ref[rows, 2 * BRANCH_W:3 * BRANCH_W])
    for r0 in chunks:
        rows = slice(r0, r0 + ROWS)
        base = SC_HALO + r0 - (SHORT_K - 1)
        acc = scw_ref[0:1, :] * mbuf[base:base + ROWS, :]
        for k in range(1, SHORT_K):
            acc = acc + scw_ref[k:k + 1, :] * mbuf[base + k:base + k + ROWS, :]
        gate = proj_ref[rows, 3 * BRANCH_W:4 * BRANCH_W]
        z_ref[3, rows, :] = (proj_ref[rows, 0:BRANCH_W] * acc * _silu(gate)).astype(BF16)

    half = D_MODEL // 2
    for c0 in (0, half):
        acc = None
        for n in range(N_BRANCH):
            gcol = N_BRANCH_COLS + n * D_MODEL + c0
            gates = jax.nn.sigmoid(_dot(hb_ref[...], win_ref[:, gcol:gcol + half]))
            term = gates * _dot(z_ref[n], wbr_ref[n, :, c0:c0 + half])
            acc = term if acc is None else acc + term
        mg_ref[:, c0:c0 + half] = acc.astype(BF16)

    for c0 in (0, half):
        o_ref[:, c0:c0 + half] = x_ref[:, c0:c0 + half] + _dot(
            mg_ref[...], wo_ref[:, c0:c0 + half])
    if final_norm:
        for r0 in chunks:
            rows = slice(r0, r0 + ROWS)
            o_ref[rows, :] = _rms_norm_rows(o_ref[rows, :], fg_ref[...])

    ybuf[0:CONV_HALO, :] = ybuf[TILE:TILE + CONV_HALO, :]
    pbuf[0:POOL_HALO, :] = pbuf[TILE:TILE + POOL_HALO, :]
    l1buf[0:POOL_HALO, :] = l1buf[TILE:TILE + POOL_HALO, :]
    l2buf[0:POOL_HALO, :] = l2buf[TILE:TILE + POOL_HALO, :]
    l3buf[0:POOL_HALO, :] = l3buf[TILE:TILE + POOL_HALO, :]
    mbuf[0:SC_HALO, :] = mbuf[TILE:TILE + SC_HALO, :]


def _resident(shape):
    zeros = (0,) * len(shape)
    return pl.BlockSpec(shape, lambda b, j: zeros, pipeline_mode=pl.Buffered(1))


def _layer_call(x, ng, win, poolw, pscale, convw, convb, clng, clnb, slng, slnb,
                sguw, sgub, scw, wbr, wo, fg, *, final_norm):
    batch, seq, d = x.shape
    assert d == D_MODEL and seq % TILE == 0 and TILE % CHUNK == 0 and TILE % ROWS == 0
    x_spec = pl.BlockSpec((None, TILE, D_MODEL), lambda b, j: (b, j, 0))
    weights = (ng, win, poolw, pscale, convw, convb, clng, clnb, slng, slnb,
               sguw, sgub, scw, wbr, wo, fg)
    scratch = [
        pltpu.VMEM((TILE, D_MODEL), BF16),
        pltpu.VMEM((TILE, 4 * BRANCH_W), F32),
        pltpu.VMEM((N_BRANCH, TILE, BRANCH_W), BF16),
        pltpu.VMEM((TILE, D_MODEL), BF16),
        pltpu.VMEM((CONV_HALO + TILE, BRANCH_W), F32),
        pltpu.VMEM((POOL_HALO + TILE, BRANCH_W), F32),
        pltpu.VMEM((POOL_HALO + TILE, 4 * GROUP_W), F32),
        pltpu.VMEM((POOL_HALO + TILE, 3 * GROUP_W), F32),
        pltpu.VMEM((POOL_HALO + TILE, 2 * GROUP_W), F32),
        pltpu.VMEM((SC_HALO + TILE, BRANCH_W), F32),
        pltpu.VMEM((TILE, BRANCH_W), BF16),
        pltpu.VMEM((TILE, BRANCH_W), BF16),
    ]
    return pl.pallas_call(
        functools.partial(_layer_kernel, final_norm=final_norm),
        out_shape=jax.ShapeDtypeStruct(x.shape, x.dtype),
        grid=(batch, seq // TILE),
        in_specs=[x_spec] + [_resident(w.shape) for w in weights],
        out_specs=x_spec,
        scratch_shapes=scratch,
        compiler_params=pltpu.CompilerParams(
            dimension_semantics=("arbitrary", "arbitrary"),
            vmem_limit_bytes=VMEM_LIMIT_BYTES),
        name="mixer_layer_final" if final_norm else "mixer_layer",
    )(x, *weights)


def kernel(x, norm_g, w_in, pool_w, pool_scale, conv_w, conv_b, conv_ln_g, conv_ln_b,
           sgu_ln_g, sgu_ln_b, sgu_w, sgu_b, sc_w, w_branch, w_o, final_g):
    depth = w_in.shape[0]
    row = lambda v: v.reshape(1, -1).astype(F32)
    for l in range(depth):
        sgub = jnp.repeat(sgu_b[l].T.astype(F32), GROUP_W, axis=1)
        x = _layer_call(
            x, row(norm_g[l]), w_in[l].astype(BF16), pool_w[l].astype(BF16),
            row(pool_scale[l]), conv_w[l].astype(F32), row(conv_b[l]),
            row(conv_ln_g[l]), row(conv_ln_b[l]), row(sgu_ln_g[l]), row(sgu_ln_b[l]),
            sgu_w[l].astype(F32), sgub, sc_w[l].astype(F32),
            w_branch[l].astype(BF16), w_o[l].astype(BF16), row(final_g),
            final_norm=(l == depth - 1))
    return x
```

```python
import functools

import jax
import jax.numpy as jnp
from jax import lax
from jax.experimental import pallas as pl
from jax.experimental.pallas import tpu as pltpu

D_MODEL = 1024
BRANCH_W = 512
N_BRANCH = 4
N_GROUPS = 4
GROUP_W = 128
POOL_WINDOWS = (2, 4, 8, 16)
CONV_K = 31
SHORT_K = 3
CHUNK = 128
N_BRANCH_COLS = 12 * BRANCH_W
IN_COLS = N_BRANCH_COLS + N_BRANCH * D_MODEL
RMS_EPS = 1e-6
LN_EPS = 1e-5

POOL_COL = 0 * BRANCH_W
CONV_COL = 2 * BRANCH_W
SGU_COL = 5 * BRANCH_W
SC_COL = 8 * BRANCH_W

SUBLANES = 8
LANES = 128
TILE = 256
ROWS = 32
SHIFT_ROWS = 64
CONV_HALO = 32
POOL_HALO = 16
SC_HALO = 8
VMEM_LIMIT_BYTES = 56 * 1024 * 1024

F32 = jnp.float32
BF16 = jnp.bfloat16


def _silu(v):
    return v * jax.nn.sigmoid(v)


def _layer_norm_rows(v, g, b):
    mu = jnp.mean(v, axis=-1, keepdims=True)
    d = v - mu
    var = jnp.mean(d * d, axis=-1, keepdims=True)
    return d * lax.rsqrt(var + LN_EPS) * g + b


def _rms_norm_rows(v, g):
    ms = jnp.mean(v * v, axis=-1, keepdims=True)
    return v * lax.rsqrt(ms + RMS_EPS) * g


def _dot(a, b):
    return jnp.dot(a, b, preferred_element_type=F32)


def _pack_weight_rows(w):
    wb = w.astype(BF16)
    *lead, k, n = wb.shape
    wb = jnp.swapaxes(wb.reshape(*lead, k // 2, 2, n), -1, -2)
    return lax.bitcast_convert_type(wb, jnp.uint32)


def _dot_packed(a, packed_b):
    return _dot(a, pltpu.bitcast(packed_b, BF16))


def _delay_rows(v, s):
    return pltpu.roll(v, s, axis=0)


def _layer_kernel(x_ref, ng_ref, win_ref, poolw_ref, pscale_ref, convw_ref, convb_ref,
                  clng_ref, clnb_ref, slng_ref, slnb_ref, sguw_ref, sgub_ref, scw_ref,
                  wbr_ref, wo_ref, fg_ref, o_ref,
                  hb_ref, pgate_ref, mixed_ref, cproj_ref, gproj_ref, sproj_ref,
                  z_ref, acc_ref, mg_ref, ybuf, pbuf, mbuf, vb_ref, poolb_ref, *, final_norm):
    j = pl.program_id(1)
    chunks = range(0, TILE, ROWS)
    shift_chunks = range(0, TILE, SHIFT_ROWS)
    lane_blocks = [slice(i * LANES, (i + 1) * LANES) for i in range(BRANCH_W // LANES)]
    half = D_MODEL // 2

    @pl.when(j == 0)
    def _zero_history():
        ybuf[0:CONV_HALO, :] = jnp.zeros((CONV_HALO, BRANCH_W), F32)
        pbuf[0:POOL_HALO, :] = jnp.zeros((POOL_HALO, BRANCH_W), F32)
        mbuf[0:SC_HALO, :] = jnp.zeros((SC_HALO, BRANCH_W), F32)

    def merge_branch(n, first):
        for c0 in (0, half):
            gcol = N_BRANCH_COLS + n * D_MODEL + c0
            gates = jax.nn.sigmoid(_dot_packed(hb_ref[...], win_ref[:, gcol:gcol + half]))
            term = gates * _dot_packed(z_ref[n], wbr_ref[n, :, c0:c0 + half])
            if first:
                acc_ref[:, c0:c0 + half] = term
            else:
                acc_ref[:, c0:c0 + half] += term

    for r0 in chunks:
        xr = x_ref[r0:r0 + ROWS, :]
        hb_ref[r0:r0 + ROWS, :] = _rms_norm_rows(xr, ng_ref[...]).astype(BF16)

    cproj_ref[...] = _dot_packed(hb_ref[...], win_ref[:, CONV_COL:CONV_COL + 3 * BRANCH_W])
    pbuf[POOL_HALO:POOL_HALO + TILE, :] = _dot_packed(
        hb_ref[...], win_ref[:, POOL_COL:POOL_COL + BRANCH_W])
    pgate_ref[...] = _dot_packed(
        hb_ref[...], win_ref[:, POOL_COL + BRANCH_W:POOL_COL + 2 * BRANCH_W])
    gproj_ref[...] = _dot_packed(hb_ref[...], win_ref[:, SGU_COL:SGU_COL + 3 * BRANCH_W])
    sproj_ref[...] = _dot_packed(hb_ref[...], win_ref[:, SC_COL:SC_COL + 4 * BRANCH_W])

    for r0 in chunks:
        rows = slice(r0, r0 + ROWS)
        a = cproj_ref[rows, 0:BRANCH_W]
        b = cproj_ref[rows, BRANCH_W:2 * BRANCH_W]
        ybuf[CONV_HALO + r0:CONV_HALO + r0 + ROWS, :] = a * jax.nn.sigmoid(b)
    n_ext = (SHIFT_ROWS + CONV_HALO) // SUBLANES
    n_out = SHIFT_ROWS // SUBLANES
    sub = lax.broadcasted_iota(jnp.int32, (n_ext - 1, SUBLANES, LANES), 1)
    for r0 in shift_chunks:
        for cols in lane_blocks:
            ext = ybuf[r0:r0 + SHIFT_ROWS + CONV_HALO, cols].reshape(n_ext, SUBLANES, LANES)
            acc = None
            for r in range(SUBLANES):
                if r == 0:
                    sh = ext
                else:
                    rot = pltpu.roll(ext, SUBLANES - r, axis=1)
                    sh = jnp.where(sub < SUBLANES - r, rot[:-1], rot[1:])
                for q in range(CONV_HALO // SUBLANES + 1):
                    k = SUBLANES * q + r - (CONV_HALO - CONV_K + 1)
                    if 0 <= k < CONV_K:
                        term = convw_ref[k, :, cols][None] * sh[q:q + n_out]
                        acc = term if acc is None else acc + term
            cproj_ref[r0:r0 + SHIFT_ROWS, cols] = acc.reshape(SHIFT_ROWS, LANES)
    for r0 in chunks:
        rows = slice(r0, r0 + ROWS)
        conv = cproj_ref[rows, 0:BRANCH_W] + convb_ref[...]
        y = _silu(_layer_norm_rows(conv, clng_ref[...], clnb_ref[...]))
        gate = cproj_ref[rows, 2 * BRANCH_W:3 * BRANCH_W]
        z_ref[1, rows, :] = (y * _silu(gate)).astype(BF16)

    for r0 in shift_chunks:
        pos = (j * TILE + r0 + 1
               + lax.broadcasted_iota(jnp.int32, (SHIFT_ROWS, LANES), 0)).astype(F32)
        for g, win in enumerate(POOL_WINDOWS):
            cols = lane_blocks[g]
            ext = pbuf[r0:r0 + POOL_HALO + SHIFT_ROWS, cols]
            s = ext
            width = 1
            while width < win:
                s = s + _delay_rows(s, width)
                width *= 2
            mean = s[POOL_HALO:] / jnp.minimum(pos, float(win))
            poolb_ref[r0:r0 + SHIFT_ROWS, cols] = (mean - ext[POOL_HALO:]).astype(BF16)
    for g in range(N_GROUPS):
        cols = lane_blocks[g]
        mixed_ref[:, cols] = _dot(poolb_ref[:, cols], poolw_ref[g])
    for r0 in chunks:
        rows = slice(r0, r0 + ROWS)
        z_ref[0, rows, :] = (mixed_ref[rows, :] * pscale_ref[...]
                             * _silu(pgate_ref[rows, :])).astype(BF16)
    merge_branch(0, first=True)

    for r0 in chunks:
        rows = slice(r0, r0 + ROWS)
        v = gproj_ref[rows, BRANCH_W:2 * BRANCH_W]
        vb_ref[rows, :] = _layer_norm_rows(v, slng_ref[...], slnb_ref[...]).astype(BF16)
    tri = (lax.broadcasted_iota(jnp.int32, (CHUNK, CHUNK), 0)
           >= lax.broadcasted_iota(jnp.int32, (CHUNK, CHUNK), 1))
    for g in range(N_GROUPS):
        cols = lane_blocks[g]
        wmask = jnp.where(tri, sguw_ref[g], 0.0).astype(BF16)
        for n in range(TILE // CHUNK):
            rows = slice(n * CHUNK, (n + 1) * CHUNK)
            sp = _dot(wmask, vb_ref[rows, cols]) + sgub_ref[:, cols]
            u = gproj_ref[rows, cols]
            gate = gproj_ref[rows, 2 * BRANCH_W + g * GROUP_W:2 * BRANCH_W + (g + 1) * GROUP_W]
            z_ref[2, rows, cols] = (u * sp * _silu(gate)).astype(BF16)
    merge_branch(2, first=False)

    for r0 in chunks:
        rows = slice(r0, r0 + ROWS)
        mbuf[SC_HALO + r0:SC_HALO + r0 + ROWS, :] = (
            sproj_ref[rows, BRANCH_W:2 * BRANCH_W] * sproj_ref[rows, 2 * BRANCH_W:3 * BRANCH_W])
    for r0 in shift_chunks:
        rows = slice(r0, r0 + SHIFT_ROWS)
        for cols in lane_blocks:
            ext = mbuf[r0:r0 + SC_HALO + SHIFT_ROWS, cols]
            conv = scw_ref[SHORT_K - 1:SHORT_K, cols] * ext[SC_HALO:]
            for k in range(SHORT_K - 1):
                conv = conv + (scw_ref[k:k + 1, cols]
                               * _delay_rows(ext, SHORT_K - 1 - k)[SC_HALO:])
            z_ref[3, rows, cols] = (
                sproj_ref[rows, cols] * conv
                * _silu(sproj_ref[rows, 3 * BRANCH_W + cols.start:3 * BRANCH_W + cols.stop])
            ).astype(BF16)
    merge_branch(3, first=False)
    merge_branch(1, first=False)

    for r0 in chunks:
        rows = slice(r0, r0 + ROWS)
        mg_ref[rows, :] = acc_ref[rows, :].astype(BF16)
    for c0 in (0, half):
        o_ref[:, c0:c0 + half] = x_ref[:, c0:c0 + half] + _dot_packed(
            mg_ref[...], wo_ref[:, c0:c0 + half])
    if final_norm:
        for r0 in chunks:
            rows = slice(r0, r0 + ROWS)
            o_ref[rows, :] = _rms_norm_rows(o_ref[rows, :], fg_ref[...])

    ybuf[0:CONV_HALO, :] = ybuf[TILE:TILE + CONV_HALO, :]
    pbuf[0:POOL_HALO, :] = pbuf[TILE:TILE + POOL_HALO, :]
    mbuf[0:SC_HALO, :] = mbuf[TILE:TILE + SC_HALO, :]


def _resident(shape):
    zeros = (0,) * len(shape)
    return pl.BlockSpec(shape, lambda b, j: zeros, pipeline_mode=pl.Buffered(1))


def _layer_call(x, ng, win, poolw, pscale, convw, convb, clng, clnb, slng, slnb,
                sguw, sgub, scw, wbr, wo, fg, *, final_norm):
    batch, seq, d = x.shape
    assert d == D_MODEL and seq % TILE == 0
    assert TILE % CHUNK == 0 and TILE % ROWS == 0 and TILE % SHIFT_ROWS == 0
    x_spec = pl.BlockSpec((None, TILE, D_MODEL), lambda b, j: (b, j, 0))
    weights = (ng, win, poolw, pscale, convw, convb, clng, clnb, slng, slnb,
               sguw, sgub, scw, wbr, wo, fg)
    scratch = [
        pltpu.VMEM((TILE, D_MODEL), BF16),
        pltpu.VMEM((TILE, BRANCH_W), F32),
        pltpu.VMEM((TILE, BRANCH_W), F32),
        pltpu.VMEM((TILE, 3 * BRANCH_W), F32),
        pltpu.VMEM((TILE, 3 * BRANCH_W), F32),
        pltpu.VMEM((TILE, 4 * BRANCH_W), F32),
        pltpu.VMEM((N_BRANCH, TILE, BRANCH_W), BF16),
        pltpu.VMEM((TILE, D_MODEL), F32),
        pltpu.VMEM((TILE, D_MODEL), BF16),
        pltpu.VMEM((CONV_HALO + TILE, BRANCH_W), F32),
        pltpu.VMEM((POOL_HALO + TILE, BRANCH_W), F32),
        pltpu.VMEM((SC_HALO + TILE, BRANCH_W), F32),
        pltpu.VMEM((TILE, BRANCH_W), BF16),
        pltpu.VMEM((TILE, BRANCH_W), BF16),
    ]
    return pl.pallas_call(
        functools.partial(_layer_kernel, final_norm=final_norm),
        out_shape=jax.ShapeDtypeStruct(x.shape, x.dtype),
        grid=(batch, seq // TILE),
        in_specs=[x_spec] + [_resident(w.shape) for w in weights],
        out_specs=x_spec,
        scratch_shapes=scratch,
        compiler_params=pltpu.CompilerParams(
            dimension_semantics=("arbitrary", "arbitrary"),
            vmem_limit_bytes=VMEM_LIMIT_BYTES),
        name="mixer_layer_final" if final_norm else "mixer_layer",
    )(x, *weights)


def kernel(x, norm_g, w_in, pool_w, pool_scale, conv_w, conv_b, conv_ln_g, conv_ln_b,
           sgu_ln_g, sgu_ln_b, sgu_w, sgu_b, sc_w, w_branch, w_o, final_g):
    depth = w_in.shape[0]
    row = lambda v: v.reshape(1, -1).astype(F32)
    for l in range(depth):
        sgub = jnp.repeat(sgu_b[l].T.astype(F32), GROUP_W, axis=1)
        convw = jnp.broadcast_to(conv_w[l].astype(F32)[:, None, :], (CONV_K, SUBLANES, BRANCH_W))
        x = _layer_call(
            x, row(norm_g[l]), _pack_weight_rows(w_in[l]), pool_w[l].astype(BF16),
            row(pool_scale[l]), convw, row(conv_b[l]),
            row(conv_ln_g[l]), row(conv_ln_b[l]), row(sgu_ln_g[l]), row(sgu_ln_b[l]),
            sgu_w[l].astype(F32), sgub, sc_w[l].astype(F32),
            _pack_weight_rows(w_branch[l]), _pack_weight_rows(w_o[l]), row(final_g),
            final_norm=(l == depth - 1))
    return x
```

```python
import functools

import jax
import jax.numpy as jnp
from jax import lax
from jax.experimental import pallas as pl
from jax.experimental.pallas import tpu as pltpu

D_MODEL = 1024
BRANCH_W = 512
N_BRANCH = 4
N_GROUPS = 4
GROUP_W = 128
POOL_WINDOWS = (2, 4, 8, 16)
CONV_K = 31
SHORT_K = 3
CHUNK = 128
N_BRANCH_COLS = 12 * BRANCH_W
IN_COLS = N_BRANCH_COLS + N_BRANCH * D_MODEL
RMS_EPS = 1e-6
LN_EPS = 1e-5

POOL_COL = 0 * BRANCH_W
CONV_COL = 2 * BRANCH_W
SGU_COL = 5 * BRANCH_W
SC_COL = 8 * BRANCH_W

SUBLANES = 8
LANES = 128
TILE = 256
ROWS = 32
MERGE_COLS = 512
SHIFT_ROWS = 64
CONV_HALO = 32
POOL_HALO = 16
SC_HALO = 8
VMEM_LIMIT_BYTES = 56 * 1024 * 1024

F32 = jnp.float32
BF16 = jnp.bfloat16


def _silu(v):
    return v * jax.nn.sigmoid(v)


def _layer_norm_rows(v, g, b):
    mu = jnp.mean(v, axis=-1, keepdims=True)
    d = v - mu
    var = jnp.mean(d * d, axis=-1, keepdims=True)
    return d * lax.rsqrt(var + LN_EPS) * g + b


def _rms_norm_rows(v, g):
    ms = jnp.mean(v * v, axis=-1, keepdims=True)
    return v * lax.rsqrt(ms + RMS_EPS) * g


def _dot(a, b):
    return jnp.dot(a, b, preferred_element_type=F32)


def _pack_weight_rows(w):
    bits = lax.bitcast_convert_type(w.astype(BF16), jnp.uint16).astype(jnp.uint32)
    return bits[..., 0::2, :] | (bits[..., 1::2, :] << 16)


def _dot_packed(a, packed_b):
    return _dot(a, pltpu.bitcast(packed_b, BF16))


def _delay_rows(v, s):
    return pltpu.roll(v, s, axis=0)


def _layer_kernel(x_ref, ng_ref, win_ref, poolw_ref, pscale_ref, convw_ref, convb_ref,
                  clng_ref, clnb_ref, slng_ref, slnb_ref, sguw_ref, sgub_ref, scw_ref,
                  wbr_ref, wo_ref, fg_ref, o_ref,
                  hb_ref, pgate_ref, mixed_ref, cproj_ref, gproj_ref, sproj_ref,
                  z_ref, acc_ref, mg_ref, ybuf, pbuf, mbuf, vb_ref, poolb_ref, *, final_norm):
    j = pl.program_id(1)
    chunks = range(0, TILE, ROWS)
    shift_chunks = range(0, TILE, SHIFT_ROWS)
    lane_blocks = [slice(i * LANES, (i + 1) * LANES) for i in range(BRANCH_W // LANES)]
    half = MERGE_COLS
    col_chunks = range(0, D_MODEL, MERGE_COLS)

    @pl.when(j == 0)
    def _zero_history():
        ybuf[0:CONV_HALO, :] = jnp.zeros((CONV_HALO, BRANCH_W), F32)
        pbuf[0:POOL_HALO, :] = jnp.zeros((POOL_HALO, BRANCH_W), F32)
        mbuf[0:SC_HALO, :] = jnp.zeros((SC_HALO, BRANCH_W), F32)

    def merge_branch(n, first):
        for c0 in col_chunks:
            gcol = N_BRANCH_COLS + n * D_MODEL + c0
            gates = jax.nn.sigmoid(_dot_packed(hb_ref[...], win_ref[:, gcol:gcol + half]))
            term = gates * _dot_packed(z_ref[n], wbr_ref[n, :, c0:c0 + half])
            if first:
                acc_ref[:, c0:c0 + half] = term
            else:
                acc_ref[:, c0:c0 + half] += term

    for r0 in chunks:
        xr = x_ref[r0:r0 + ROWS, :]
        hb_ref[r0:r0 + ROWS, :] = _rms_norm_rows(xr, ng_ref[...]).astype(BF16)

    cproj_ref[...] = _dot_packed(hb_ref[...], win_ref[:, CONV_COL:CONV_COL + 3 * BRANCH_W])
    pbuf[POOL_HALO:POOL_HALO + TILE, :] = _dot_packed(
        hb_ref[...], win_ref[:, POOL_COL:POOL_COL + BRANCH_W])
    pgate_ref[...] = _dot_packed(
        hb_ref[...], win_ref[:, POOL_COL + BRANCH_W:POOL_COL + 2 * BRANCH_W])
    gproj_ref[...] = _dot_packed(hb_ref[...], win_ref[:, SGU_COL:SGU_COL + 3 * BRANCH_W])
    sproj_ref[...] = _dot_packed(hb_ref[...], win_ref[:, SC_COL:SC_COL + 4 * BRANCH_W])

    for r0 in chunks:
        rows = slice(r0, r0 + ROWS)
        a = cproj_ref[rows, 0:BRANCH_W]
        b = cproj_ref[rows, BRANCH_W:2 * BRANCH_W]
        ybuf[CONV_HALO + r0:CONV_HALO + r0 + ROWS, :] = a * jax.nn.sigmoid(b)
    n_ext = (SHIFT_ROWS + CONV_HALO) // SUBLANES
    n_out = SHIFT_ROWS // SUBLANES
    sub = lax.broadcasted_iota(jnp.int32, (n_ext - 1, SUBLANES, LANES), 1)
    for r0 in shift_chunks:
        for cols in lane_blocks:
            ext = ybuf[r0:r0 + SHIFT_ROWS + CONV_HALO, cols].reshape(n_ext, SUBLANES, LANES)
            acc = None
            for r in range(SUBLANES):
                if r == 0:
                    sh = ext
                else:
                    rot = pltpu.roll(ext, SUBLANES - r, axis=1)
                    sh = jnp.where(sub < SUBLANES - r, rot[:-1], rot[1:])
                for q in range(CONV_HALO // SUBLANES + 1):
                    k = SUBLANES * q + r - (CONV_HALO - CONV_K + 1)
                    if 0 <= k < CONV_K:
                        term = convw_ref[k, :, cols][None] * sh[q:q + n_out]
                        acc = term if acc is None else acc + term
            cproj_ref[r0:r0 + SHIFT_ROWS, cols] = acc.reshape(SHIFT_ROWS, LANES)
    for r0 in chunks:
        rows = slice(r0, r0 + ROWS)
        conv = cproj_ref[rows, 0:BRANCH_W] + convb_ref[...]
        y = _silu(_layer_norm_rows(conv, clng_ref[...], clnb_ref[...]))
        gate = cproj_ref[rows, 2 * BRANCH_W:3 * BRANCH_W]
        z_ref[1, rows, :] = (y * _silu(gate)).astype(BF16)

    for r0 in shift_chunks:
        pos = (j * TILE + r0 + 1
               + lax.broadcasted_iota(jnp.int32, (SHIFT_ROWS, LANES), 0)).astype(F32)
        for g, win in enumerate(POOL_WINDOWS):
            cols = lane_blocks[g]
            ext = pbuf[r0:r0 + POOL_HALO + SHIFT_ROWS, cols]
            s = ext
            width = 1
            while width < win:
                s = s + _delay_rows(s, width)
                width *= 2
            mean = s[POOL_HALO:] / jnp.minimum(pos, float(win))
            poolb_ref[r0:r0 + SHIFT_ROWS, cols] = (mean - ext[POOL_HALO:]).astype(BF16)
    for g in range(N_GROUPS):
        cols = lane_blocks[g]
        mixed_ref[:, cols] = _dot(poolb_ref[:, cols], poolw_ref[g])
    for r0 in chunks:
        rows = slice(r0, r0 + ROWS)
        z_ref[0, rows, :] = (mixed_ref[rows, :] * pscale_ref[...]
                             * _silu(pgate_ref[rows, :])).astype(BF16)
    merge_branch(0, first=True)

    for r0 in chunks:
        rows = slice(r0, r0 + ROWS)
        v = gproj_ref[rows, BRANCH_W:2 * BRANCH_W]
        vb_ref[rows, :] = _layer_norm_rows(v, slng_ref[...], slnb_ref[...]).astype(BF16)
    tri = (lax.broadcasted_iota(jnp.int32, (CHUNK, CHUNK), 0)
           >= lax.broadcasted_iota(jnp.int32, (CHUNK, CHUNK), 1))
    for g in range(N_GROUPS):
        cols = lane_blocks[g]
        wmask = jnp.where(tri, sguw_ref[g], 0.0).astype(BF16)
        for n in range(TILE // CHUNK):
            rows = slice(n * CHUNK, (n + 1) * CHUNK)
            sp = _dot(wmask, vb_ref[rows, cols]) + sgub_ref[:, cols]
            u = gproj_ref[rows, cols]
            gate = gproj_ref[rows, 2 * BRANCH_W + g * GROUP_W:2 * BRANCH_W + (g + 1) * GROUP_W]
            z_ref[2, rows, cols] = (u * sp * _silu(gate)).astype(BF16)
    merge_branch(2, first=False)

    for r0 in chunks:
        rows = slice(r0, r0 + ROWS)
        mbuf[SC_HALO + r0:SC_HALO + r0 + ROWS, :] = (
            sproj_ref[rows, BRANCH_W:2 * BRANCH_W] * sproj_ref[rows, 2 * BRANCH_W:3 * BRANCH_W])
    for r0 in shift_chunks:
        rows = slice(r0, r0 + SHIFT_ROWS)
        for cols in lane_blocks:
            ext = mbuf[r0:r0 + SC_HALO + SHIFT_ROWS, cols]
            conv = scw_ref[SHORT_K - 1:SHORT_K, cols] * ext[SC_HALO:]
            for k in range(SHORT_K - 1):
                conv = conv + (scw_ref[k:k + 1, cols]
                               * _delay_rows(ext, SHORT_K - 1 - k)[SC_HALO:])
            z_ref[3, rows, cols] = (
                sproj_ref[rows, cols] * conv
                * _silu(sproj_ref[rows, 3 * BRANCH_W + cols.start:3 * BRANCH_W + cols.stop])
            ).astype(BF16)
    merge_branch(3, first=False)
    merge_branch(1, first=False)

    for r0 in chunks:
        rows = slice(r0, r0 + ROWS)
        mg_ref[rows, :] = acc_ref[rows, :].astype(BF16)
    for c0 in col_chunks:
        o_ref[:, c0:c0 + half] = x_ref[:, c0:c0 + half] + _dot_packed(
            mg_ref[...], wo_ref[:, c0:c0 + half])
    if final_norm:
        for r0 in chunks:
            rows = slice(r0, r0 + ROWS)
            o_ref[rows, :] = _rms_norm_rows(o_ref[rows, :], fg_ref[...])

    ybuf[0:CONV_HALO, :] = ybuf[TILE:TILE + CONV_HALO, :]
    pbuf[0:POOL_HALO, :] = pbuf[TILE:TILE + POOL_HALO, :]
    mbuf[0:SC_HALO, :] = mbuf[TILE:TILE + SC_HALO, :]


def _resident(shape):
    zeros = (0,) * len(shape)
    return pl.BlockSpec(shape, lambda b, j: zeros, pipeline_mode=pl.Buffered(1))


def _layer_call(x, ng, win, poolw, pscale, convw, convb, clng, clnb, slng, slnb,
                sguw, sgub, scw, wbr, wo, fg, *, final_norm):
    batch, seq, d = x.shape
    assert d == D_MODEL and seq % TILE == 0
    assert TILE % CHUNK == 0 and TILE % ROWS == 0 and TILE % SHIFT_ROWS == 0
    x_spec = pl.BlockSpec((None, TILE, D_MODEL), lambda b, j: (b, j, 0))
    weights = (ng, win, poolw, pscale, convw, convb, clng, clnb, slng, slnb,
               sguw, sgub, scw, wbr, wo, fg)
    scratch = [
        pltpu.VMEM((TILE, D_MODEL), BF16),
        pltpu.VMEM((TILE, BRANCH_W), F32),
        pltpu.VMEM((TILE, BRANCH_W), F32),
        pltpu.VMEM((TILE, 3 * BRANCH_W), F32),
        pltpu.VMEM((TILE, 3 * BRANCH_W), F32),
        pltpu.VMEM((TILE, 4 * BRANCH_W), F32),
        pltpu.VMEM((N_BRANCH, TILE, BRANCH_W), BF16),
        pltpu.VMEM((TILE, D_MODEL), F32),
        pltpu.VMEM((TILE, D_MODEL), BF16),
        pltpu.VMEM((CONV_HALO + TILE, BRANCH_W), F32),
        pltpu.VMEM((POOL_HALO + TILE, BRANCH_W), F32),
        pltpu.VMEM((SC_HALO + TILE, BRANCH_W), F32),
        pltpu.VMEM((TILE, BRANCH_W), BF16),
        pltpu.VMEM((TILE, BRANCH_W), BF16),
    ]
    return pl.pallas_call(
        functools.partial(_layer_kernel, final_norm=final_norm),
        out_shape=jax.ShapeDtypeStruct(x.shape, x.dtype),
        grid=(batch, seq // TILE),
        in_specs=[x_spec] + [_resident(w.shape) for w in weights],
        out_specs=x_spec,
        scratch_shapes=scratch,
        compiler_params=pltpu.CompilerParams(
            dimension_semantics=("arbitrary", "arbitrary"),
            vmem_limit_bytes=VMEM_LIMIT_BYTES),
        name="mixer_layer_final" if final_norm else "mixer_layer",
    )(x, *weights)


def kernel(x, norm_g, w_in, pool_w, pool_scale, conv_w, conv_b, conv_ln_g, conv_ln_b,
           sgu_ln_g, sgu_ln_b, sgu_w, sgu_b, sc_w, w_branch, w_o, final_g):
    depth = w_in.shape[0]
    row = lambda v: v.reshape(1, -1).astype(F32)
    for l in range(depth):
        sgub = jnp.repeat(sgu_b[l].T.astype(F32), GROUP_W, axis=1)
        convw = jnp.broadcast_to(conv_w[l].astype(F32)[:, None, :], (CONV_K, SUBLANES, BRANCH_W))
        x = _layer_call(
            x, row(norm_g[l]), _pack_weight_rows(w_in[l]), pool_w[l].astype(BF16),
            row(pool_scale[l]), convw, row(conv_b[l]),
            row(conv_ln_g[l]), row(conv_ln_b[l]), row(sgu_ln_g[l]), row(sgu_ln_b[l]),
            sgu_w[l].astype(F32), sgub, sc_w[l].astype(F32),
            _pack_weight_rows(w_branch[l]), _pack_weight_rows(w_o[l]), row(final_g),
            final_norm=(l == depth - 1))
    return x
```

```python
import functools

import jax
import jax.numpy as jnp
from jax import lax
from jax.experimental import pallas as pl
from jax.experimental.pallas import tpu as pltpu

D_MODEL = 1024
BRANCH_W = 512
N_BRANCH = 4
N_GROUPS = 4
GROUP_W = 128
POOL_WINDOWS = (2, 4, 8, 16)
CONV_K = 31
SHORT_K = 3
CHUNK = 128
N_BRANCH_COLS = 12 * BRANCH_W
IN_COLS = N_BRANCH_COLS + N_BRANCH * D_MODEL
RMS_EPS = 1e-6
LN_EPS = 1e-5

POOL_COL = 0 * BRANCH_W
CONV_COL = 2 * BRANCH_W
SGU_COL = 5 * BRANCH_W
SC_COL = 8 * BRANCH_W

SUBLANES = 8
LANES = 128
TILE = 256
ROWS = 32
MERGE_COLS = 512
SHIFT_ROWS = 64
CONV_HALO = 32
POOL_HALO = 16
SC_HALO = 8
VMEM_LIMIT_BYTES = 56 * 1024 * 1024
PACK_ROWS = 512
PACK_COLS = 1024

F32 = jnp.float32
BF16 = jnp.bfloat16


def _silu(v):
    return v * jax.nn.sigmoid(v)


def _layer_norm_rows(v, g, b):
    mu = jnp.mean(v, axis=-1, keepdims=True)
    d = v - mu
    var = jnp.mean(d * d, axis=-1, keepdims=True)
    return d * lax.rsqrt(var + LN_EPS) * g + b


def _rms_norm_rows(v, g):
    ms = jnp.mean(v * v, axis=-1, keepdims=True)
    return v * lax.rsqrt(ms + RMS_EPS) * g


def _dot(a, b):
    return jnp.dot(a, b, preferred_element_type=F32)


def _pack_weight_rows(w, name):
    *lead, k, n = w.shape
    rows = 1
    for d in lead:
        rows *= d
    rows *= k
    assert k % 2 == 0 and rows % PACK_ROWS == 0 and n % PACK_COLS == 0
    packed = pl.pallas_call(
        _pack_rows_kernel,
        out_shape=jax.ShapeDtypeStruct((rows // 2, n), jnp.uint32),
        grid=(rows // PACK_ROWS, n // PACK_COLS),
        in_specs=[pl.BlockSpec((PACK_ROWS, PACK_COLS), lambda i, j: (i, j))],
        out_specs=pl.BlockSpec((PACK_ROWS // 2, PACK_COLS), lambda i, j: (i, j)),
        name=name,
    )(w.reshape(rows, n))
    return packed.reshape(*lead, k // 2, n)


def _pack_rows_kernel(w_ref, o_ref):
    o_ref[...] = pltpu.bitcast(w_ref[...].astype(BF16), jnp.uint32)


def _dot_packed(a, packed_b):
    return _dot(a, pltpu.bitcast(packed_b, BF16))


def _delay_rows(v, s):
    return pltpu.roll(v, s, axis=0)


def _layer_kernel(x_ref, ng_ref, win_ref, poolw_ref, pscale_ref, convw_ref, convb_ref,
                  clng_ref, clnb_ref, slng_ref, slnb_ref, sguw_ref, sgub_ref, scw_ref,
                  wbr_ref, wo_ref, fg_ref, o_ref,
                  hb_ref, pgate_ref, mixed_ref, cproj_ref, gproj_ref, sproj_ref,
                  z_ref, acc_ref, mg_ref, ybuf, pbuf, mbuf, vb_ref, poolb_ref, *, final_norm):
    j = pl.program_id(1)
    chunks = range(0, TILE, ROWS)
    shift_chunks = range(0, TILE, SHIFT_ROWS)
    lane_blocks = [slice(i * LANES, (i + 1) * LANES) for i in range(BRANCH_W // LANES)]
    half = MERGE_COLS
    col_chunks = range(0, D_MODEL, MERGE_COLS)

    @pl.when(j == 0)
    def _zero_history():
        ybuf[0:CONV_HALO, :] = jnp.zeros((CONV_HALO, BRANCH_W), F32)
        pbuf[0:POOL_HALO, :] = jnp.zeros((POOL_HALO, BRANCH_W), F32)
        mbuf[0:SC_HALO, :] = jnp.zeros((SC_HALO, BRANCH_W), F32)

    def merge_branch(n, first):
        for c0 in col_chunks:
            gcol = N_BRANCH_COLS + n * D_MODEL + c0
            gates = jax.nn.sigmoid(_dot_packed(hb_ref[...], win_ref[:, gcol:gcol + half]))
            term = gates * _dot_packed(z_ref[n], wbr_ref[n, :, c0:c0 + half])
            if first:
                acc_ref[:, c0:c0 + half] = term
            else:
                acc_ref[:, c0:c0 + half] += term

    for r0 in chunks:
        xr = x_ref[r0:r0 + ROWS, :]
        hb_ref[r0:r0 + ROWS, :] = _rms_norm_rows(xr, ng_ref[...]).astype(BF16)

    cproj_ref[...] = _dot_packed(hb_ref[...], win_ref[:, CONV_COL:CONV_COL + 3 * BRANCH_W])
    pbuf[POOL_HALO:POOL_HALO + TILE, :] = _dot_packed(
        hb_ref[...], win_ref[:, POOL_COL:POOL_COL + BRANCH_W])
    pgate_ref[...] = _dot_packed(
        hb_ref[...], win_ref[:, POOL_COL + BRANCH_W:POOL_COL + 2 * BRANCH_W])
    gproj_ref[...] = _dot_packed(hb_ref[...], win_ref[:, SGU_COL:SGU_COL + 3 * BRANCH_W])
    sproj_ref[...] = _dot_packed(hb_ref[...], win_ref[:, SC_COL:SC_COL + 4 * BRANCH_W])

    for r0 in chunks:
        rows = slice(r0, r0 + ROWS)
        a = cproj_ref[rows, 0:BRANCH_W]
        b = cproj_ref[rows, BRANCH_W:2 * BRANCH_W]
        ybuf[CONV_HALO + r0:CONV_HALO + r0 + ROWS, :] = a * jax.nn.sigmoid(b)
    n_ext = (SHIFT_ROWS + CONV_HALO) // SUBLANES
    n_out = SHIFT_ROWS // SUBLANES
    sub = lax.broadcasted_iota(jnp.int32, (n_ext - 1, SUBLANES, LANES), 1)
    for r0 in shift_chunks:
        for cols in lane_blocks:
            ext = ybuf[r0:r0 + SHIFT_ROWS + CONV_HALO, cols].reshape(n_ext, SUBLANES, LANES)
            acc = None
            for r in range(SUBLANES):
                if r == 0:
                    sh = ext
                else:
                    rot = pltpu.roll(ext, SUBLANES - r, axis=1)
                    sh = jnp.where(sub < SUBLANES - r, rot[:-1], rot[1:])
                for q in range(CONV_HALO // SUBLANES + 1):
                    k = SUBLANES * q + r - (CONV_HALO - CONV_K + 1)
                    if 0 <= k < CONV_K:
                        term = convw_ref[k, :, cols][None] * sh[q:q + n_out]
                        acc = term if acc is None else acc + term
            cproj_ref[r0:r0 + SHIFT_ROWS, cols] = acc.reshape(SHIFT_ROWS, LANES)
    for r0 in chunks:
        rows = slice(r0, r0 + ROWS)
        conv = cproj_ref[rows, 0:BRANCH_W] + convb_ref[...]
        y = _silu(_layer_norm_rows(conv, clng_ref[...], clnb_ref[...]))
        gate = cproj_ref[rows, 2 * BRANCH_W:3 * BRANCH_W]
        z_ref[1, rows, :] = (y * _silu(gate)).astype(BF16)

    for r0 in shift_chunks:
        pos = (j * TILE + r0 + 1
               + lax.broadcasted_iota(jnp.int32, (SHIFT_ROWS, LANES), 0)).astype(F32)
        for g, win in enumerate(POOL_WINDOWS):
            cols = lane_blocks[g]
            ext = pbuf[r0:r0 + POOL_HALO + SHIFT_ROWS, cols]
            s = ext
            width = 1
            while width < win:
                s = s + _delay_rows(s, width)
                width *= 2
            mean = s[POOL_HALO:] / jnp.minimum(pos, float(win))
            poolb_ref[r0:r0 + SHIFT_ROWS, cols] = (mean - ext[POOL_HALO:]).astype(BF16)
    for g in range(N_GROUPS):
        cols = lane_blocks[g]
        mixed_ref[:, cols] = _dot(poolb_ref[:, cols], poolw_ref[g])
    for r0 in chunks:
        rows = slice(r0, r0 + ROWS)
        z_ref[0, rows, :] = (mixed_ref[rows, :] * pscale_ref[...]
                             * _silu(pgate_ref[rows, :])).astype(BF16)
    merge_branch(0, first=True)

    for r0 in chunks:
        rows = slice(r0, r0 + ROWS)
        v = gproj_ref[rows, BRANCH_W:2 * BRANCH_W]
        vb_ref[rows, :] = _layer_norm_rows(v, slng_ref[...], slnb_ref[...]).astype(BF16)
    tri = (lax.broadcasted_iota(jnp.int32, (CHUNK, CHUNK), 0)
           >= lax.broadcasted_iota(jnp.int32, (CHUNK, CHUNK), 1))
    for g in range(N_GROUPS):
        cols = lane_blocks[g]
        wmask = jnp.where(tri, sguw_ref[g], 0.0).astype(BF16)
        for n in range(TILE // CHUNK):
            rows = slice(n * CHUNK, (n + 1) * CHUNK)
            sp = _dot(wmask, vb_ref[rows, cols]) + sgub_ref[:, cols]
            u = gproj_ref[rows, cols]
            gate = gproj_ref[rows, 2 * BRANCH_W + g * GROUP_W:2 * BRANCH_W + (g + 1) * GROUP_W]
            z_ref[2, rows, cols] = (u * sp * _silu(gate)).astype(BF16)
    merge_branch(2, first=False)

    for r0 in chunks:
        rows = slice(r0, r0 + ROWS)
        mbuf[SC_HALO + r0:SC_HALO + r0 + ROWS, :] = (
            sproj_ref[rows, BRANCH_W:2 * BRANCH_W] * sproj_ref[rows, 2 * BRANCH_W:3 * BRANCH_W])
    for r0 in shift_chunks:
        rows = slice(r0, r0 + SHIFT_ROWS)
        for cols in lane_blocks:
            ext = mbuf[r0:r0 + SC_HALO + SHIFT_ROWS, cols]
            conv = scw_ref[SHORT_K - 1:SHORT_K, cols] * ext[SC_HALO:]
            for k in range(SHORT_K - 1):
                conv = conv + (scw_ref[k:k + 1, cols]
                               * _delay_rows(ext, SHORT_K - 1 - k)[SC_HALO:])
            z_ref[3, rows, cols] = (
                sproj_ref[rows, cols] * conv
                * _silu(sproj_ref[rows, 3 * BRANCH_W + cols.start:3 * BRANCH_W + cols.stop])
            ).astype(BF16)
    merge_branch(3, first=False)
    merge_branch(1, first=False)

    for r0 in chunks:
        rows = slice(r0, r0 + ROWS)
        mg_ref[rows, :] = acc_ref[rows, :].astype(BF16)
    for c0 in col_chunks:
        o_ref[:, c0:c0 + half] = x_ref[:, c0:c0 + half] + _dot_packed(
            mg_ref[...], wo_ref[:, c0:c0 + half])
    if final_norm:
        for r0 in chunks:
            rows = slice(r0, r0 + ROWS)
            o_ref[rows, :] = _rms_norm_rows(o_ref[rows, :], fg_ref[...])

    ybuf[0:CONV_HALO, :] = ybuf[TILE:TILE + CONV_HALO, :]
    pbuf[0:POOL_HALO, :] = pbuf[TILE:TILE + POOL_HALO, :]
    mbuf[0:SC_HALO, :] = mbuf[TILE:TILE + SC_HALO, :]


def _resident(shape):
    zeros = (0,) * len(shape)
    return pl.BlockSpec(shape, lambda b, j: zeros, pipeline_mode=pl.Buffered(1))


def _resident_layer(stacked_shape, layer):
    index = (layer,) + (0,) * (len(stacked_shape) - 1)
    return pl.BlockSpec((None,) + tuple(stacked_shape[1:]), lambda b, j: index,
                        pipeline_mode=pl.Buffered(1))


def _layer_call(x, ng, win, poolw, pscale, convw, convb, clng, clnb, slng, slnb,
                sguw, sgub, scw, wbr, wo, fg, *, layer, final_norm):
    batch, seq, d = x.shape
    assert d == D_MODEL and seq % TILE == 0
    assert TILE % CHUNK == 0 and TILE % ROWS == 0 and TILE % SHIFT_ROWS == 0
    x_spec = pl.BlockSpec((None, TILE, D_MODEL), lambda b, j: (b, j, 0))
    weights = (ng, win, poolw, pscale, convw, convb, clng, clnb, slng, slnb,
               sguw, sgub, scw, wbr, wo, fg)
    stacked = (win, wbr, wo)
    weight_specs = [
        _resident_layer(w.shape, layer) if any(w is s for s in stacked) else _resident(w.shape)
        for w in weights]
    scratch = [
        pltpu.VMEM((TILE, D_MODEL), BF16),
        pltpu.VMEM((TILE, BRANCH_W), F32),
        pltpu.VMEM((TILE, BRANCH_W), F32),
        pltpu.VMEM((TILE, 3 * BRANCH_W), F32),
        pltpu.VMEM((TILE, 3 * BRANCH_W), F32),
        pltpu.VMEM((TILE, 4 * BRANCH_W), F32),
        pltpu.VMEM((N_BRANCH, TILE, BRANCH_W), BF16),
        pltpu.VMEM((TILE, D_MODEL), F32),
        pltpu.VMEM((TILE, D_MODEL), BF16),
        pltpu.VMEM((CONV_HALO + TILE, BRANCH_W), F32),
        pltpu.VMEM((POOL_HALO + TILE, BRANCH_W), F32),
        pltpu.VMEM((SC_HALO + TILE, BRANCH_W), F32),
        pltpu.VMEM((TILE, BRANCH_W), BF16),
        pltpu.VMEM((TILE, BRANCH_W), BF16),
    ]
    return pl.pallas_call(
        functools.partial(_layer_kernel, final_norm=final_norm),
        out_shape=jax.ShapeDtypeStruct(x.shape, x.dtype),
        grid=(batch, seq // TILE),
        in_specs=[x_spec] + weight_specs,
        out_specs=x_spec,
        scratch_shapes=scratch,
        compiler_params=pltpu.CompilerParams(
            dimension_semantics=("arbitrary", "arbitrary"),
            vmem_limit_bytes=VMEM_LIMIT_BYTES),
        name="mixer_layer_final" if final_norm else "mixer_layer",
    )(x, *weights)


def kernel(x, norm_g, w_in, pool_w, pool_scale, conv_w, conv_b, conv_ln_g, conv_ln_b,
           sgu_ln_g, sgu_ln_b, sgu_w, sgu_b, sc_w, w_branch, w_o, final_g):
    depth = w_in.shape[0]
    row = lambda v: v.reshape(1, -1).astype(F32)
    win_packed = _pack_weight_rows(w_in, "pack_w_in")
    wbr_packed = _pack_weight_rows(w_branch, "pack_w_branch")
    wo_packed = _pack_weight_rows(w_o, "pack_w_o")
    for l in range(depth):
        sgub = jnp.repeat(sgu_b[l].T.astype(F32), GROUP_W, axis=1)
        convw = jnp.broadcast_to(conv_w[l].astype(F32)[:, None, :], (CONV_K, SUBLANES, BRANCH_W))
        x = _layer_call(
            x, row(norm_g[l]), win_packed, pool_w[l].astype(BF16),
            row(pool_scale[l]), convw, row(conv_b[l]),
            row(conv_ln_g[l]), row(conv_ln_b[l]), row(sgu_ln_g[l]), row(sgu_ln_b[l]),
            sgu_w[l].astype(F32), sgub, sc_w[l].astype(F32),
            wbr_packed, wo_packed, row(final_g),
            layer=l, final_norm=(l == depth - 1))
    return x
```

```python
import functools

import jax
import jax.numpy as jnp
from jax import lax
from jax.experimental import pallas as pl
from jax.experimental.pallas import tpu as pltpu

D_MODEL = 1024
BRANCH_W = 512
N_BRANCH = 4
N_GROUPS = 4
GROUP_W = 128
POOL_WINDOWS = (2, 4, 8, 16)
CONV_K = 31
SHORT_K = 3
CHUNK = 128
N_BRANCH_COLS = 12 * BRANCH_W
IN_COLS = N_BRANCH_COLS + N_BRANCH * D_MODEL
RMS_EPS = 1e-6
LN_EPS = 1e-5

POOL_COL = 0 * BRANCH_W
CONV_COL = 2 * BRANCH_W
SGU_COL = 5 * BRANCH_W
SC_COL = 8 * BRANCH_W

SUBLANES = 8
LANES = 128
TILE = 512
ROWS = 32
MERGE_COLS = 512
SHIFT_ROWS = 64
CONV_HALO = 32
POOL_HALO = 16
SC_HALO = 8
VMEM_LIMIT_BYTES = 56 * 1024 * 1024
PACK_ROWS = 512
PACK_COLS = 1024

F32 = jnp.float32
BF16 = jnp.bfloat16


def _silu(v):
    return v * jax.nn.sigmoid(v)


def _layer_norm_rows(v, g, b):
    mu = jnp.mean(v, axis=-1, keepdims=True)
    d = v - mu
    var = jnp.mean(d * d, axis=-1, keepdims=True)
    return d * lax.rsqrt(var + LN_EPS) * g + b


def _rms_norm_rows(v, g):
    ms = jnp.mean(v * v, axis=-1, keepdims=True)
    return v * lax.rsqrt(ms + RMS_EPS) * g


def _dot(a, b):
    return jnp.dot(a, b, preferred_element_type=F32)


def _pack_weight_rows(w, name):
    *lead, k, n = w.shape
    rows = 1
    for d in lead:
        rows *= d
    rows *= k
    assert k % 2 == 0 and rows % PACK_ROWS == 0 and n % PACK_COLS == 0
    packed = pl.pallas_call(
        _pack_rows_kernel,
        out_shape=jax.ShapeDtypeStruct((rows // 2, n), jnp.uint32),
        grid=(rows // PACK_ROWS, n // PACK_COLS),
        in_specs=[pl.BlockSpec((PACK_ROWS, PACK_COLS), lambda i, j: (i, j))],
        out_specs=pl.BlockSpec((PACK_ROWS // 2, PACK_COLS), lambda i, j: (i, j)),
        name=name,
    )(w.reshape(rows, n))
    return packed.reshape(*lead, k // 2, n)


def _pack_rows_kernel(w_ref, o_ref):
    o_ref[...] = pltpu.bitcast(w_ref[...].astype(BF16), jnp.uint32)


def _dot_packed(a, packed_b):
    return _dot(a, pltpu.bitcast(packed_b, BF16))


def _delay_rows(v, s):
    return pltpu.roll(v, s, axis=0)


def _layer_kernel(x_ref, ng_ref, win_ref, poolw_ref, pscale_ref, convw_ref, convb_ref,
                  clng_ref, clnb_ref, slng_ref, slnb_ref, sguw_ref, sgub_ref, scw_ref,
                  wbr_ref, wo_ref, fg_ref, o_ref,
                  hb_ref, pgate_ref, mixed_ref, cproj_ref, gproj_ref, sproj_ref,
                  z_ref, acc_ref, mg_ref, ybuf, pbuf, mbuf, vb_ref, poolb_ref, *, final_norm):
    j = pl.program_id(1)
    chunks = range(0, TILE, ROWS)
    shift_chunks = range(0, TILE, SHIFT_ROWS)
    lane_blocks = [slice(i * LANES, (i + 1) * LANES) for i in range(BRANCH_W // LANES)]
    half = MERGE_COLS
    col_chunks = range(0, D_MODEL, MERGE_COLS)

    @pl.when(j == 0)
    def _zero_history():
        ybuf[0:CONV_HALO, :] = jnp.zeros((CONV_HALO, BRANCH_W), F32)
        pbuf[0:POOL_HALO, :] = jnp.zeros((POOL_HALO, BRANCH_W), F32)
        mbuf[0:SC_HALO, :] = jnp.zeros((SC_HALO, BRANCH_W), F32)

    def merge_branch(n, first):
        for c0 in col_chunks:
            gcol = N_BRANCH_COLS + n * D_MODEL + c0
            gates = jax.nn.sigmoid(_dot_packed(hb_ref[...], win_ref[:, gcol:gcol + half]))
            term = gates * _dot_packed(z_ref[n], wbr_ref[n, :, c0:c0 + half])
            if first:
                acc_ref[:, c0:c0 + half] = term
            else:
                acc_ref[:, c0:c0 + half] += term

    for r0 in chunks:
        xr = x_ref[r0:r0 + ROWS, :]
        hb_ref[r0:r0 + ROWS, :] = _rms_norm_rows(xr, ng_ref[...]).astype(BF16)

    cproj_ref[...] = _dot_packed(hb_ref[...], win_ref[:, CONV_COL:CONV_COL + 3 * BRANCH_W])
    pbuf[POOL_HALO:POOL_HALO + TILE, :] = _dot_packed(
        hb_ref[...], win_ref[:, POOL_COL:POOL_COL + BRANCH_W])
    pgate_ref[...] = _dot_packed(
        hb_ref[...], win_ref[:, POOL_COL + BRANCH_W:POOL_COL + 2 * BRANCH_W])
    gproj_ref[...] = _dot_packed(hb_ref[...], win_ref[:, SGU_COL:SGU_COL + 3 * BRANCH_W])
    sproj_ref[...] = _dot_packed(hb_ref[...], win_ref[:, SC_COL:SC_COL + 4 * BRANCH_W])

    for r0 in chunks:
        rows = slice(r0, r0 + ROWS)
        a = cproj_ref[rows, 0:BRANCH_W]
        b = cproj_ref[rows, BRANCH_W:2 * BRANCH_W]
        ybuf[CONV_HALO + r0:CONV_HALO + r0 + ROWS, :] = a * jax.nn.sigmoid(b)
    n_ext = (SHIFT_ROWS + CONV_HALO) // SUBLANES
    n_out = SHIFT_ROWS // SUBLANES
    sub = lax.broadcasted_iota(jnp.int32, (n_ext - 1, SUBLANES, LANES), 1)
    for r0 in shift_chunks:
        for cols in lane_blocks:
            ext = ybuf[r0:r0 + SHIFT_ROWS + CONV_HALO, cols].reshape(n_ext, SUBLANES, LANES)
            acc = None
            for r in range(SUBLANES):
                if r == 0:
                    sh = ext
                else:
                    rot = pltpu.roll(ext, SUBLANES - r, axis=1)
                    sh = jnp.where(sub < SUBLANES - r, rot[:-1], rot[1:])
                for q in range(CONV_HALO // SUBLANES + 1):
                    k = SUBLANES * q + r - (CONV_HALO - CONV_K + 1)
                    if 0 <= k < CONV_K:
                        term = convw_ref[k, :, cols][None] * sh[q:q + n_out]
                        acc = term if acc is None else acc + term
            cproj_ref[r0:r0 + SHIFT_ROWS, cols] = acc.reshape(SHIFT_ROWS, LANES)
    for r0 in chunks:
        rows = slice(r0, r0 + ROWS)
        conv = cproj_ref[rows, 0:BRANCH_W] + convb_ref[...]
        y = _silu(_layer_norm_rows(conv, clng_ref[...], clnb_ref[...]))
        gate = cproj_ref[rows, 2 * BRANCH_W:3 * BRANCH_W]
        z_ref[1, rows, :] = (y * _silu(gate)).astype(BF16)

    for r0 in shift_chunks:
        pos = (j * TILE + r0 + 1
               + lax.broadcasted_iota(jnp.int32, (SHIFT_ROWS, LANES), 0)).astype(F32)
        for g, win in enumerate(POOL_WINDOWS):
            cols = lane_blocks[g]
            ext = pbuf[r0:r0 + POOL_HALO + SHIFT_ROWS, cols]
            s = ext
            width = 1
            while width < win:
                s = s + _delay_rows(s, width)
                width *= 2
            mean = s[POOL_HALO:] / jnp.minimum(pos, float(win))
            poolb_ref[r0:r0 + SHIFT_ROWS, cols] = (mean - ext[POOL_HALO:]).astype(BF16)
    for g in range(N_GROUPS):
        cols = lane_blocks[g]
        mixed_ref[:, cols] = _dot(poolb_ref[:, cols], poolw_ref[g])
    for r0 in chunks:
        rows = slice(r0, r0 + ROWS)
        z_ref[0, rows, :] = (mixed_ref[rows, :] * pscale_ref[...]
                             * _silu(pgate_ref[rows, :])).astype(BF16)
    merge_branch(0, first=True)

    for r0 in chunks:
        rows = slice(r0, r0 + ROWS)
        v = gproj_ref[rows, BRANCH_W:2 * BRANCH_W]
        vb_ref[rows, :] = _layer_norm_rows(v, slng_ref[...], slnb_ref[...]).astype(BF16)
    tri = (lax.broadcasted_iota(jnp.int32, (CHUNK, CHUNK), 0)
           >= lax.broadcasted_iota(jnp.int32, (CHUNK, CHUNK), 1))
    for g in range(N_GROUPS):
        cols = lane_blocks[g]
        wmask = jnp.where(tri, sguw_ref[g], 0.0).astype(BF16)
        for n in range(TILE // CHUNK):
            rows = slice(n * CHUNK, (n + 1) * CHUNK)
            sp = _dot(wmask, vb_ref[rows, cols]) + sgub_ref[:, cols]
            u = gproj_ref[rows, cols]
            gate = gproj_ref[rows, 2 * BRANCH_W + g * GROUP_W:2 * BRANCH_W + (g + 1) * GROUP_W]
            z_ref[2, rows, cols] = (u * sp * _silu(gate)).astype(BF16)
    merge_branch(2, first=False)

    for r0 in chunks:
        rows = slice(r0, r0 + ROWS)
        mbuf[SC_HALO + r0:SC_HALO + r0 + ROWS, :] = (
            sproj_ref[rows, BRANCH_W:2 * BRANCH_W] * sproj_ref[rows, 2 * BRANCH_W:3 * BRANCH_W])
    for r0 in shift_chunks:
        rows = slice(r0, r0 + SHIFT_ROWS)
        for cols in lane_blocks:
            ext = mbuf[r0:r0 + SC_HALO + SHIFT_ROWS, cols]
            conv = scw_ref[SHORT_K - 1:SHORT_K, cols] * ext[SC_HALO:]
            for k in range(SHORT_K - 1):
                conv = conv + (scw_ref[k:k + 1, cols]
                               * _delay_rows(ext, SHORT_K - 1 - k)[SC_HALO:])
            z_ref[3, rows, cols] = (
                sproj_ref[rows, cols] * conv
                * _silu(sproj_ref[rows, 3 * BRANCH_W + cols.start:3 * BRANCH_W + cols.stop])
            ).astype(BF16)
    merge_branch(3, first=False)
    merge_branch(1, first=False)

    for r0 in chunks:
        rows = slice(r0, r0 + ROWS)
        mg_ref[rows, :] = acc_ref[rows, :].astype(BF16)
    for c0 in col_chunks:
        o_ref[:, c0:c0 + half] = x_ref[:, c0:c0 + half] + _dot_packed(
            mg_ref[...], wo_ref[:, c0:c0 + half])
    if final_norm:
        for r0 in chunks:
            rows = slice(r0, r0 + ROWS)
            o_ref[rows, :] = _rms_norm_rows(o_ref[rows, :], fg_ref[...])

    ybuf[0:CONV_HALO, :] = ybuf[TILE:TILE + CONV_HALO, :]
    pbuf[0:POOL_HALO, :] = pbuf[TILE:TILE + POOL_HALO, :]
    mbuf[0:SC_HALO, :] = mbuf[TILE:TILE + SC_HALO, :]


def _resident(shape):
    zeros = (0,) * len(shape)
    return pl.BlockSpec(shape, lambda b, j: zeros, pipeline_mode=pl.Buffered(1))


def _resident_layer(stacked_shape, layer):
    index = (layer,) + (0,) * (len(stacked_shape) - 1)
    return pl.BlockSpec((None,) + tuple(stacked_shape[1:]), lambda b, j: index,
                        pipeline_mode=pl.Buffered(1))


def _layer_call(x, ng, win, poolw, pscale, convw, convb, clng, clnb, slng, slnb,
                sguw, sgub, scw, wbr, wo, fg, *, layer, final_norm):
    batch, seq, d = x.shape
    assert d == D_MODEL and seq % TILE == 0
    assert TILE % CHUNK == 0 and TILE % ROWS == 0 and TILE % SHIFT_ROWS == 0
    x_spec = pl.BlockSpec((None, TILE, D_MODEL), lambda b, j: (b, j, 0))
    weights = (ng, win, poolw, pscale, convw, convb, clng, clnb, slng, slnb,
               sguw, sgub, scw, wbr, wo, fg)
    stacked = (win, wbr, wo)
    weight_specs = [
        _resident_layer(w.shape, layer) if any(w is s for s in stacked) else _resident(w.shape)
        for w in weights]
    scratch = [
        pltpu.VMEM((TILE, D_MODEL), BF16),
        pltpu.VMEM((TILE, BRANCH_W), F32),
        pltpu.VMEM((TILE, BRANCH_W), F32),
        pltpu.VMEM((TILE, 3 * BRANCH_W), F32),
        pltpu.VMEM((TILE, 3 * BRANCH_W), F32),
        pltpu.VMEM((TILE, 4 * BRANCH_W), F32),
        pltpu.VMEM((N_BRANCH, TILE, BRANCH_W), BF16),
        pltpu.VMEM((TILE, D_MODEL), F32),
        pltpu.VMEM((TILE, D_MODEL), BF16),
        pltpu.VMEM((CONV_HALO + TILE, BRANCH_W), F32),
        pltpu.VMEM((POOL_HALO + TILE, BRANCH_W), F32),
        pltpu.VMEM((SC_HALO + TILE, BRANCH_W), F32),
        pltpu.VMEM((TILE, BRANCH_W), BF16),
        pltpu.VMEM((TILE, BRANCH_W), BF16),
    ]
    return pl.pallas_call(
        functools.partial(_layer_kernel, final_norm=final_norm),
        out_shape=jax.ShapeDtypeStruct(x.shape, x.dtype),
        grid=(batch, seq // TILE),
        in_specs=[x_spec] + weight_specs,
        out_specs=x_spec,
        scratch_shapes=scratch,
        compiler_params=pltpu.CompilerParams(
            dimension_semantics=("arbitrary", "arbitrary"),
            vmem_limit_bytes=VMEM_LIMIT_BYTES),
        name="mixer_layer_final" if final_norm else "mixer_layer",
    )(x, *weights)


def kernel(x, norm_g, w_in, pool_w, pool_scale, conv_w, conv_b, conv_ln_g, conv_ln_b,
           sgu_ln_g, sgu_ln_b, sgu_w, sgu_b, sc_w, w_branch, w_o, final_g):
    depth = w_in.shape[0]
    row = lambda v: v.reshape(1, -1).astype(F32)
    win_packed = _pack_weight_rows(w_in, "pack_w_in")
    wbr_packed = _pack_weight_rows(w_branch, "pack_w_branch")
    wo_packed = _pack_weight_rows(w_o, "pack_w_o")
    for l in range(depth):
        sgub = jnp.repeat(sgu_b[l].T.astype(F32), GROUP_W, axis=1)
        convw = jnp.broadcast_to(conv_w[l].astype(F32)[:, None, :], (CONV_K, SUBLANES, BRANCH_W))
        x = _layer_call(
            x, row(norm_g[l]), win_packed, pool_w[l].astype(BF16),
            row(pool_scale[l]), convw, row(conv_b[l]),
            row(conv_ln_g[l]), row(conv_ln_b[l]), row(sgu_ln_g[l]), row(sgu_ln_b[l]),
            sgu_w[l].astype(F32), sgub, sc_w[l].astype(F32),
            wbr_packed, wo_packed, row(final_g),
            layer=l, final_norm=(l == depth - 1))
    return x
```
